```python
import math
import jax
import jax.numpy as jnp
from jax import lax
import numpy as np

D_MODEL = 1024
BATCH = 8
SEQ = 2048
DEPTH = 2
DEC_BATCH = 128
DEC_SEQ = 4
PAST_LEN = 16384
PAGE_SIZE = 128

RET_HEADS = 8
RET_DK = 64
RET_DV = 128
GLA_HEADS = 4
GLA_DK = 128
GLA_DV = 256
GLA_GATE_RANK = 16
GLA_GATE_NORM = 16.0
RWKV_HEADS = 16
RWKV_HD = 64
RWKV_W = RWKV_HEADS * RWKV_HD
RWKV_DECAY_RANK = 64
RWKV_A_RANK = 64
RWKV_GATE_RANK = 128
RWKV_GN_EPS = 64e-5
D_FF = 2816
N_EXPERTS = 8
TOP_K = 2
N_BRANCH = 3
CHUNK = 64
ROPE_BASE = 10000.0
EPS = 1e-6
RET_PROJ = 2 * RET_HEADS * RET_DK + 2 * RET_HEADS * RET_DV
GLA_PROJ = 2 * GLA_HEADS * GLA_DK + 2 * GLA_HEADS * GLA_DV + GLA_GATE_RANK
RWKV_PROJ = 3 * RWKV_W + RWKV_DECAY_RANK + RWKV_A_RANK + RWKV_GATE_RANK
MERGE_COLS = N_BRANCH * D_MODEL
IN_COLS = RET_PROJ + GLA_PROJ + RWKV_PROJ + MERGE_COLS

kernel_name = 'hybrid_retention_rwkv7_gla_decoder_step'


def split_cols(x, widths):
    idx = [int(v) for v in np.cumsum(widths)[:-1]]
    return jnp.split(x, idx, axis=-1)


def rmsnorm(x, g):
    xf = x.astype(jnp.float32)
    y = xf * lax.rsqrt(jnp.mean(xf * xf, axis=-1, keepdims=True) + EPS)
    return (y * g.astype(jnp.float32)).astype(x.dtype)


def head_rmsnorm(o):
    return o * lax.rsqrt(jnp.mean(o * o, axis=-1, keepdims=True) + EPS)


def head_groupnorm(y, w, b):
    mu = jnp.mean(y, axis=-1, keepdims=True)
    yc = y - mu
    var = jnp.mean(yc * yc, axis=-1, keepdims=True)
    yn = yc * lax.rsqrt(var + RWKV_GN_EPS)
    return yn * w.reshape(RWKV_HEADS, RWKV_HD) + b.reshape(RWKV_HEADS, RWKV_HD)


def rope(x, pos):
    half = x.shape[-1] // 2
    inv = ROPE_BASE ** (-jnp.arange(half, dtype=jnp.float32) / half)
    ang = pos.astype(jnp.float32)[:, None] * inv[None, :]
    cos = jnp.cos(ang)[None, :, None, :]
    sin = jnp.sin(ang)[None, :, None, :]
    x1, x2 = x[..., :half], x[..., half:]
    return jnp.concatenate([x1 * cos - x2 * sin, x1 * sin + x2 * cos], axis=-1)


def to_chunks(x, L):
    B, T, H, d = x.shape
    return x.reshape(B, T // L, L, H, d).transpose(1, 0, 3, 2, 4)


def from_chunks(x):
    N, B, H, L, d = x.shape
    return x.transpose(1, 0, 3, 2, 4).reshape(B, N * L, H, d)


def retention_scan(q, k, v, S0, log_gamma):
    T = q.shape[1]
    L = math.gcd(T, CHUNK)
    i = jnp.arange(L, dtype=jnp.float32)
    diff = i[:, None] - i[None, :]
    dmat = jnp.where(diff >= 0, jnp.exp(log_gamma[:, None, None] * jnp.maximum(diff, 0.0)), 0.0)
    q_dec = jnp.exp(log_gamma[:, None] * (i + 1.0))[:, :, None]
    k_dec = jnp.exp(log_gamma[:, None] * (L - 1.0 - i))[:, :, None]
    c_dec = jnp.exp(log_gamma * L)[:, None, None]

    def step(S, xs):
        qb, kb, vb = xs
        sc = jnp.einsum('bhid,bhjd->bhij', qb, kb) * dmat
        o = jnp.einsum('bhij,bhjv->bhiv', sc, vb) + jnp.einsum('bhid,bhdv->bhiv', qb * q_dec, S)
        S = S * c_dec + jnp.einsum('bhjd,bhjv->bhdv', kb * k_dec, vb)
        return S, o

    S, o = lax.scan(step, S0, (to_chunks(q, L), to_chunks(k, L), to_chunks(v, L)))
    return from_chunks(o), S


def gla_scan(q, k, v, log_a, S0):
    T = q.shape[1]
    L = math.gcd(T, CHUNK)
    causal = jnp.tril(jnp.ones((L, L), dtype=bool))

    def step(S, xs):
        qb, kb, vb, gb = xs
        b = jnp.cumsum(gb, axis=2)
        rel = jnp.where(causal[:, :, None], b[:, :, :, None, :] - b[:, :, None, :, :], -jnp.inf)
        att = jnp.einsum('bhid,bhjd,bhijd->bhij', qb, kb, jnp.exp(rel))
        o = jnp.einsum('bhij,bhjv->bhiv', att, vb) + jnp.einsum('bhid,bhdv->bhiv', qb * jnp.exp(b), S)
        bl = b[:, :, -1:, :]
        S = S * jnp.exp(bl)[:, :, 0, :, None] + jnp.einsum('bhjd,bhjv->bhdv', kb * jnp.exp(bl - b), vb)
        return S, o

    xs = (to_chunks(q, L), to_chunks(k, L), to_chunks(v, L), to_chunks(log_a, L))
    S, o = lax.scan(step, S0, xs)
    return from_chunks(o), S


def rwkv7_scan(r, w, k, v, kk, a, S0):
    def step(S, xs):
        r_t, w_t, k_t, v_t, kk_t, a_t = xs
        sa = jnp.einsum('bhvk,bhk->bhv', S, -kk_t)
        S = (S * w_t[:, :, None, :] + sa[..., None] * (kk_t * a_t)[:, :, None, :]
             + v_t[..., None] * k_t[:, :, None, :])
        y = jnp.einsum('bhvk,bhk->bhv', S, r_t)
        return S, y

    xs = tuple(jnp.swapaxes(t, 0, 1) for t in (r, w, k, v, kk, a))
    S, y = lax.scan(step, S0, xs)
    return jnp.swapaxes(y, 0, 1), S


def retention_branch(p_ret, pos0, S0):
    B, T, _ = p_ret.shape
    q, k, v, g = split_cols(p_ret, (RET_HEADS * RET_DK, RET_HEADS * RET_DK, RET_HEADS * RET_DV, RET_HEADS * RET_DV))
    pos = pos0 + jnp.arange(T, dtype=jnp.int32)
    q = rope(q.reshape(B, T, RET_HEADS, RET_DK), pos)
    k = rope(k.reshape(B, T, RET_HEADS, RET_DK), pos) * (RET_DK ** -0.5)
    v = v.reshape(B, T, RET_HEADS, RET_DV)
    log_gamma = jnp.log1p(-jnp.exp2(-5.0 - jnp.arange(RET_HEADS, dtype=jnp.float32)))
    o, S = retention_scan(q, k, v, S0, log_gamma)
    o = head_rmsnorm(o).reshape(B, T, RET_HEADS * RET_DV)
    return jax.nn.silu(g) * o, S


def gla_branch(p_gla, S0, gk_up, gk_bias, norm_w):
    B, T, _ = p_gla.shape
    q, k, v, g, gk = split_cols(p_gla, (GLA_HEADS * GLA_DK, GLA_HEADS * GLA_DK, GLA_HEADS * GLA_DV, GLA_HEADS * GLA_DV, GLA_GATE_RANK))
    log_a = jax.nn.log_sigmoid(gk @ gk_up + gk_bias) / GLA_GATE_NORM
    q = q.reshape(B, T, GLA_HEADS, GLA_DK) * (GLA_DK ** -0.5)
    k = k.reshape(B, T, GLA_HEADS, GLA_DK)
    v = v.reshape(B, T, GLA_HEADS, GLA_DV)
    log_a = log_a.reshape(B, T, GLA_HEADS, GLA_DK)
    o, S = gla_scan(q, k, v, log_a, S0)
    o = (head_rmsnorm(o) * norm_w).reshape(B, T, GLA_HEADS * GLA_DV)
    return jax.nn.silu(g) * o, S


def rwkv7_branch(p_rwkv, S0, prev0, mu, w0, w_up, a0, a_up, g_up, kk_scale, ka, rk, ln_w, ln_b):
    B, T, _ = p_rwkv.shape
    prev = jnp.concatenate([prev0[:, None, :], p_rwkv[:, :-1, :]], axis=1)
    xm = p_rwkv + (prev - p_rwkv) * mu
    r, wd, k, v, ad, gd = split_cols(xm, (RWKV_W, RWKV_DECAY_RANK, RWKV_W, RWKV_W, RWKV_A_RANK, RWKV_GATE_RANK))
    w_log = -jax.nn.softplus(-(w0 + jnp.tanh(wd) @ w_up)) - 0.5
    decay = jnp.exp(-jnp.exp(w_log))
    a = jax.nn.sigmoid(a0 + ad @ a_up)
    g = jax.nn.sigmoid(gd) @ g_up
    hs = lambda t: t.reshape(B, T, RWKV_HEADS, RWKV_HD)
    kk = hs(k * kk_scale)
    kk = kk * lax.rsqrt(jnp.sum(kk * kk, axis=-1, keepdims=True) + 1e-12)
    k = k * (1.0 + (a - 1.0) * ka)
    r_h, k_h, v_h, w_h, a_h = hs(r), hs(k), hs(v), hs(decay), hs(a)
    y, S = rwkv7_scan(r_h, w_h, k_h, v_h, kk, a_h, S0)
    y = head_groupnorm(y, ln_w, ln_b)
    y = y + jnp.sum(r_h * k_h * rk, axis=-1, keepdims=True) * v_h
    return y.reshape(B, T, RWKV_W) * g, S, p_rwkv[:, -1, :]


def token_mixer(h, pos0, s_ret, s_gla, s_rwkv, s_shift, p, l):
    B, T, _ = h.shape
    f32 = jnp.float32
    proj = (h @ p['w_in'][l]).astype(f32)
    p_ret, p_gla, p_rwkv, p_merge = split_cols(proj, (RET_PROJ, GLA_PROJ, RWKV_PROJ, MERGE_COLS))
    y_ret, sr = retention_branch(p_ret, pos0, s_ret.astype(f32))
    y_gla, sg = gla_branch(p_gla, s_gla.astype(f32), p['gla_gk_up'][l], p['gla_gk_bias'][l], p['gla_norm_w'][l])
    y_rw, sw, ss = rwkv7_branch(p_rwkv, s_rwkv.astype(f32), s_shift.astype(f32), p['rwkv_mu'][l],
                                p['rwkv_w0'][l], p['rwkv_w_up'][l], p['rwkv_a0'][l], p['rwkv_a_up'][l],
                                p['rwkv_g_up'][l], p['rwkv_kk_scale'][l], p['rwkv_ka'][l], p['rwkv_rk'][l],
                                p['rwkv_ln_w'][l], p['rwkv_ln_b'][l])
    gates = jax.nn.sigmoid(p_merge).reshape(B, T, N_BRANCH, D_MODEL)
    wb = p['w_branch'][l]
    merged = (gates[:, :, 0] * (y_ret @ wb[0]) + gates[:, :, 1] * (y_gla @ wb[1])
              + gates[:, :, 2] * (y_rw @ wb[2]))
    out = merged.astype(h.dtype) @ p['w_out'][l]
    return out, sr, sg, sw, ss


def swiglu(h, w1, w3, w2):
    return (jax.nn.silu(h @ w1) * (h @ w3)) @ w2


def moe_swiglu(h, router, w1, w3, w2):
    logits = (h @ router).astype(jnp.float32)
    top_v, top_i = lax.top_k(logits, TOP_K)
    probs = jax.nn.softmax(top_v, axis=-1)
    gate = jnp.sum(jax.nn.one_hot(top_i, N_EXPERTS, dtype=jnp.float32) * probs[..., None], axis=-2)
    out = jnp.zeros(h.shape, jnp.float32)
    for e in range(N_EXPERTS):
        out = out + gate[..., e:e + 1] * swiglu(h, w1[e], w3[e], w2[e]).astype(jnp.float32)
    return out.astype(h.dtype)


def trunk(x, pos0, s_ret, s_gla, s_rwkv, s_shift, p):
    new_ret, new_gla, new_rwkv, new_shift = [], [], [], []
    for l in range(DEPTH):
        h = rmsnorm(x, p['norm_mix'][l])
        m, sr, sg, sw, ss = token_mixer(h, pos0, s_ret[l], s_gla[l], s_rwkv[l], s_shift[l], p, l)
        x = x + m.astype(x.dtype)
        h = rmsnorm(x, p['norm_ffn'][l])
        j = l // 2
        if l % 2 == 0:
            f = swiglu(h, p['ffn_w1'][j], p['ffn_w3'][j], p['ffn_w2'][j])
        else:
            f = moe_swiglu(h, p['moe_router'][j], p['moe_w1'][j], p['moe_w3'][j], p['moe_w2'][j])
        x = x + f.astype(x.dtype)
        new_ret.append(sr)
        new_gla.append(sg)
        new_rwkv.append(sw)
        new_shift.append(ss)
    y = rmsnorm(x, p['norm_final'])
    return y, jnp.stack(new_ret), jnp.stack(new_gla), jnp.stack(new_rwkv), jnp.stack(new_shift)


def setup_inputs(seed: int = 0) -> dict:
    key = jax.random.key(seed)
    ks = iter(jax.random.split(key, 40))
    nrm = lambda shape, scale: jax.random.normal(next(ks), shape, jnp.float32) * scale
    uni = lambda shape, lo, hi: jax.random.uniform(next(ks), shape, jnp.float32, lo, hi)
    n_dense = (DEPTH + 1) // 2
    n_moe = DEPTH // 2
    D = D_MODEL
    return {
        'x_prompt': nrm((BATCH, SEQ, D), 1.0),
        'x_sample': nrm((DEC_BATCH, DEC_SEQ, D), 1.0),
        'state_ret': nrm((DEPTH, DEC_BATCH, RET_HEADS, RET_DK, RET_DV), 1.0),
        'state_gla': nrm((DEPTH, DEC_BATCH, GLA_HEADS, GLA_DK, GLA_DV), 1.0),
        'state_rwkv': nrm((DEPTH, DEC_BATCH, RWKV_HEADS, RWKV_HD, RWKV_HD), 0.5),
        'state_shift': nrm((DEPTH, DEC_BATCH, RWKV_PROJ), 1.0),
        'norm_mix': 1.0 + nrm((DEPTH, D), 0.02),
        'norm_ffn': 1.0 + nrm((DEPTH, D), 0.02),
        'w_in': nrm((DEPTH, D, IN_COLS), D ** -0.5),
        'gla_gk_up': nrm((DEPTH, GLA_GATE_RANK, GLA_HEADS * GLA_DK), GLA_GATE_RANK ** -0.5),
        'gla_gk_bias': nrm((DEPTH, GLA_HEADS * GLA_DK), 0.1),
        'gla_norm_w': 1.0 + nrm((DEPTH, GLA_DV), 0.02),
        'rwkv_mu': uni((DEPTH, RWKV_PROJ), 0.0, 1.0),
        'rwkv_w0': uni((DEPTH, RWKV_W), -6.0, -1.0),
        'rwkv_w_up': nrm((DEPTH, RWKV_DECAY_RANK, RWKV_W), 0.1 * RWKV_DECAY_RANK ** -0.5),
        'rwkv_a0': nrm((DEPTH, RWKV_W), 0.1),
        'rwkv_a_up': nrm((DEPTH, RWKV_A_RANK, RWKV_W), 0.5 * RWKV_A_RANK ** -0.5),
        'rwkv_g_up': nrm((DEPTH, RWKV_GATE_RANK, RWKV_W), RWKV_GATE_RANK ** -0.5),
        'rwkv_kk_scale': 0.85 + nrm((DEPTH, RWKV_W), 0.02),
        'rwkv_ka': 1.0 + nrm((DEPTH, RWKV_W), 0.02),
        'rwkv_rk': nrm((DEPTH, RWKV_HEADS, RWKV_HD), 0.1),
        'rwkv_ln_w': 1.0 + nrm((DEPTH, RWKV_W), 0.02),
        'rwkv_ln_b': nrm((DEPTH, RWKV_W), 0.02),
        'w_branch': nrm((DEPTH, N_BRANCH, RET_HEADS * RET_DV, D), (RET_HEADS * RET_DV) ** -0.5),
        'w_out': nrm((DEPTH, D, D), D ** -0.5),
        'ffn_w1': nrm((n_dense, D, D_FF), D ** -0.5),
        'ffn_w3': nrm((n_dense, D, D_FF), D ** -0.5),
        'ffn_w2': nrm((n_dense, D_FF, D), D_FF ** -0.5),
        'moe_router': nrm((n_moe, D, N_EXPERTS), D ** -0.5),
        'moe_w1': nrm((n_moe, N_EXPERTS, D, D_FF), D ** -0.5),
        'moe_w3': nrm((n_moe, N_EXPERTS, D, D_FF), D ** -0.5),
        'moe_w2': nrm((n_moe, N_EXPERTS, D_FF, D), D_FF ** -0.5),
        'norm_final': 1.0 + nrm((D,), 0.02),
    }


def reference(x_prompt, x_sample, state_ret, state_gla, state_rwkv, state_shift, norm_mix, norm_ffn, w_in,
              gla_gk_up, gla_gk_bias, gla_norm_w, rwkv_mu, rwkv_w0, rwkv_w_up, rwkv_a0, rwkv_a_up, rwkv_g_up,
              rwkv_kk_scale, rwkv_ka, rwkv_rk, rwkv_ln_w, rwkv_ln_b, w_branch, w_out, ffn_w1, ffn_w3, ffn_w2,
              moe_router, moe_w1, moe_w3, moe_w2, norm_final):
    p = {
        'norm_mix': norm_mix, 'norm_ffn': norm_ffn, 'w_in': w_in,
        'gla_gk_up': gla_gk_up, 'gla_gk_bias': gla_gk_bias, 'gla_norm_w': gla_norm_w,
        'rwkv_mu': rwkv_mu, 'rwkv_w0': rwkv_w0, 'rwkv_w_up': rwkv_w_up, 'rwkv_a0': rwkv_a0,
        'rwkv_a_up': rwkv_a_up, 'rwkv_g_up': rwkv_g_up, 'rwkv_kk_scale': rwkv_kk_scale, 'rwkv_ka': rwkv_ka,
        'rwkv_rk': rwkv_rk, 'rwkv_ln_w': rwkv_ln_w, 'rwkv_ln_b': rwkv_ln_b,
        'w_branch': w_branch, 'w_out': w_out,
        'ffn_w1': ffn_w1, 'ffn_w3': ffn_w3, 'ffn_w2': ffn_w2,
        'moe_router': moe_router, 'moe_w1': moe_w1, 'moe_w3': moe_w3, 'moe_w2': moe_w2,
        'norm_final': norm_final,
    }
    f32 = jnp.float32
    bp = x_prompt.shape[0]
    z_ret = jnp.zeros((DEPTH, bp, RET_HEADS, RET_DK, RET_DV), f32)
    z_gla = jnp.zeros((DEPTH, bp, GLA_HEADS, GLA_DK, GLA_DV), f32)
    z_rwkv = jnp.zeros((DEPTH, bp, RWKV_HEADS, RWKV_HD, RWKV_HD), f32)
    z_shift = jnp.zeros((DEPTH, bp, RWKV_PROJ), f32)
    y_prompt, ret_p, gla_p, rwkv_p, shift_p = trunk(x_prompt, 0, z_ret, z_gla, z_rwkv, z_shift, p)
    y_sample, ret_s, gla_s, rwkv_s, shift_s = trunk(x_sample, PAST_LEN, state_ret, state_gla, state_rwkv, state_shift, p)
    return (y_prompt, y_sample, ret_p, gla_p, rwkv_p, shift_p, ret_s, gla_s, rwkv_s, shift_s)
```

```python
import functools
import math

import jax
import jax.numpy as jnp
import numpy as np
from jax import lax
from jax.experimental import pallas as pl
from jax.experimental.pallas import tpu as pltpu

F32 = jnp.float32
BF16 = jnp.bfloat16

V7X_LANES = 128
V7X_SUBLANES = 8
V7X_VMEM_LIMIT = 56 * 1024 * 1024

D_MODEL = 1024
RET_HEADS, RET_DK, RET_DV = 8, 64, 128
GLA_HEADS, GLA_DK, GLA_DV = 4, 128, 256
GLA_GATE_RANK, GLA_GATE_NORM = 16, 16.0
RWKV_HEADS, RWKV_HD = 16, 64
RWKV_W = RWKV_HEADS * RWKV_HD
RWKV_DECAY_RANK, RWKV_A_RANK, RWKV_GATE_RANK = 64, 64, 128
RWKV_GN_EPS = 64e-5
D_FF, N_EXPERTS = 2816, 8
ROPE_BASE = 10000.0
EPS = 1e-6
RET_PROJ = 2 * RET_HEADS * RET_DK + 2 * RET_HEADS * RET_DV
GLA_PROJ = 2 * GLA_HEADS * GLA_DK + 2 * GLA_HEADS * GLA_DV + GLA_GATE_RANK
GLA_PROJ_PAD = GLA_PROJ - GLA_GATE_RANK + V7X_LANES
RWKV_PROJ = 3 * RWKV_W + RWKV_DECAY_RANK + RWKV_A_RANK + RWKV_GATE_RANK
MERGE_COLS = 3 * D_MODEL

PROMPT_CHUNK_RET = 256
PROMPT_CHUNK = 64
SAMPLE_CHUNK = 8


def _cparams(n_grid):
    return pltpu.CompilerParams(dimension_semantics=("arbitrary",) * n_grid,
                                vmem_limit_bytes=V7X_VMEM_LIMIT)


def _dot(a, b):
    return jnp.dot(a, b, preferred_element_type=F32)


def _dot_nt(a, b):
    return lax.dot_general(a, b, (((1,), (1,)), ((), ())), preferred_element_type=F32)


def _dot_tn(a, b):
    return lax.dot_general(a, b, (((0,), (0,)), ((), ())), preferred_element_type=F32)


def _split3(x):
    x1 = x.astype(BF16)
    r1 = x - x1.astype(F32)
    x2 = r1.astype(BF16)
    x3 = (r1 - x2.astype(F32)).astype(BF16)
    return x1, x2, x3


def _dot3_left(m, x):
    x1, x2, x3 = _split3(x)
    return _dot(m, x1) + _dot(m, x2) + _dot(m, x3)


def _dot3_right(x, m):
    x1, x2, x3 = _split3(x)
    return _dot(x1, m) + _dot(x2, m) + _dot(x3, m)


def _sigmoid(x):
    return 1.0 / (1.0 + jnp.exp(-x))


def _silu(x):
    return x * _sigmoid(x)


def _log1p_exp_neg_abs(x):
    return jnp.log(1.0 + jnp.exp(-jnp.abs(x)))


def _norm_matmul_kernel(x_ref, g_ref, w_ref, o_ref, h_ref):
    @pl.when(pl.program_id(1) == 0)
    def _():
        x = x_ref[...]
        ms = jnp.mean(x * x, axis=-1, keepdims=True)
        h_ref[...] = (x * lax.rsqrt(ms + EPS) * g_ref[...]).astype(BF16)

    o_ref[...] = _dot(h_ref[...], w_ref[...]).astype(o_ref.dtype)


def _norm_matmul(x, g, w, *, bm, bn, out_dtype=F32):
    m, d = x.shape
    n = w.shape[1]
    assert m % bm == 0 and n % bn == 0
    return pl.pallas_call(
        _norm_matmul_kernel,
        grid=(m // bm, n // bn),
        in_specs=[pl.BlockSpec((bm, d), lambda i, j: (i, 0)),
                  pl.BlockSpec((1, d), lambda i, j: (0, 0)),
                  pl.BlockSpec((d, bn), lambda i, j: (0, j))],
        out_specs=pl.BlockSpec((bm, bn), lambda i, j: (i, j)),
        out_shape=jax.ShapeDtypeStruct((m, n), out_dtype),
        scratch_shapes=[pltpu.VMEM((bm, d), BF16)],
        compiler_params=_cparams(2),
        name="norm_matmul",
    )(x, g.reshape(1, d), w)


def _rmsnorm_kernel(x_ref, g_ref, o_ref):
    x = x_ref[...]
    ms = jnp.mean(x * x, axis=-1, keepdims=True)
    o_ref[...] = x * lax.rsqrt(ms + EPS) * g_ref[...]


def _rmsnorm(x, g, *, bm):
    m, d = x.shape
    return pl.pallas_call(
        _rmsnorm_kernel,
        grid=(m // bm,),
        in_specs=[pl.BlockSpec((bm, d), lambda i: (i, 0)), pl.BlockSpec((1, d), lambda i: (0, 0))],
        out_specs=pl.BlockSpec((bm, d), lambda i: (i, 0)),
        out_shape=jax.ShapeDtypeStruct((m, d), F32),
        compiler_params=_cparams(1),
        name="final_rmsnorm",
    )(x, g.reshape(1, d))


def _ret_kernel(*refs, has_state):
    if has_state:
        p_ref, cos_ref, sin_ref, qdec_ref, kdec_ref, dmat_ref, cdec_ref, s0_ref, y_ref, s_ref = refs
    else:
        p_ref, cos_ref, sin_ref, qdec_ref, kdec_ref, dmat_ref, cdec_ref, y_ref, s_ref = refs
        s0_ref = None

    @pl.when(pl.program_id(1) == 0)
    def _():
        s_ref[...] = s0_ref[...] if has_state else jnp.zeros(s_ref.shape, F32)

    qk_w = RET_HEADS * RET_DK
    reps = qk_w // V7X_LANES
    cos = jnp.concatenate([cos_ref[...]] * reps, axis=1)
    sin = jnp.concatenate([sin_ref[...]] * reps, axis=1)
    lane = lax.broadcasted_iota(jnp.int32, cos.shape, 1)
    first_half = (lane & (RET_DK - 1)) < (RET_DK // 2)

    def rope(x):
        swapped = jnp.where(first_half, pltpu.roll(x, qk_w - RET_DK // 2, 1), pltpu.roll(x, RET_DK // 2, 1))
        return x * cos + swapped * sin

    qr = rope(p_ref[:, 0:qk_w])
    kr = rope(p_ref[:, qk_w:2 * qk_w]) * (RET_DK ** -0.5)
    qb, kb = qr.astype(BF16), kr.astype(BF16)
    qd = (qr * qdec_ref[...]).astype(BF16)
    kd = (kr * kdec_ref[...]).astype(BF16)
    v0 = 2 * qk_w
    g0 = v0 + RET_HEADS * RET_DV
    for h in range(RET_HEADS):
        sl = slice(h * RET_DK, (h + 1) * RET_DK)
        vs = slice(h * RET_DV, (h + 1) * RET_DV)
        v = p_ref[:, v0 + h * RET_DV:v0 + (h + 1) * RET_DV].astype(BF16)
        sc = _dot_nt(qb[:, sl], kb[:, sl]) * dmat_ref[h]
        s_old = s_ref[h]
        o = _dot(sc.astype(BF16), v) + _dot(qd[:, sl], s_old.astype(BF16))
        s_ref[h] = s_old * cdec_ref[h] + _dot_tn(kd[:, sl], v)
        o = o * lax.rsqrt(jnp.mean(o * o, axis=-1, keepdims=True) + EPS)
        g = p_ref[:, g0 + h * RET_DV:g0 + (h + 1) * RET_DV]
        y_ref[:, vs] = (_silu(g) * o).astype(y_ref.dtype)


def _ret_tables(chunk, n_valid, t_total, pos0):
    log_gamma = jnp.log1p(-jnp.exp2(-5.0 - jnp.arange(RET_HEADS, dtype=F32)))
    i = jnp.arange(chunk, dtype=F32)
    diff = i[:, None] - i[None, :]
    dmat = jnp.where(diff >= 0, jnp.exp(log_gamma[:, None, None] * jnp.maximum(diff, 0.0)), 0.0)
    q_dec = jnp.exp(log_gamma[:, None] * (i + 1.0))
    k_dec = jnp.where(i < n_valid, jnp.exp(log_gamma[:, None] * jnp.maximum(n_valid - 1.0 - i, 0.0)), 0.0)
    c_dec = jnp.exp(log_gamma * n_valid)
    spread = lambda t: jnp.repeat(t.T, RET_DK, axis=1)
    half = RET_DK // 2
    inv = ROPE_BASE ** (-jnp.arange(half, dtype=F32) / half)
    pos = (pos0 + jnp.arange(t_total, dtype=jnp.int32)).astype(F32)
    ang = pos[:, None] * inv[None, :]
    cos, sin = jnp.cos(ang), jnp.sin(ang)
    per_lane = V7X_LANES // RET_DK
    cos_t = jnp.tile(jnp.concatenate([cos, cos], axis=1), (1, per_lane))
    sin_t = jnp.tile(jnp.concatenate([-sin, sin], axis=1), (1, per_lane))
    cdec = jnp.broadcast_to(c_dec[:, None, None], (RET_HEADS, 1, RET_DV))
    return cos_t, sin_t, spread(q_dec), spread(k_dec), dmat, cdec


def _ret_branch(p, s0, *, n_seq, t_seq, chunk, n_valid, pos0):
    has_state = s0 is not None
    n_chunks = t_seq // chunk
    cos_t, sin_t, qdec, kdec, dmat, cdec = _ret_tables(chunk, n_valid, t_seq, pos0)
    qk_w = RET_HEADS * RET_DK
    const2 = lambda shape: pl.BlockSpec(shape, lambda b, c: (0,) * len(shape))
    in_specs = [pl.BlockSpec((chunk, RET_PROJ), lambda b, c: (b * n_chunks + c, 0)),
                pl.BlockSpec((chunk, V7X_LANES), lambda b, c: (c, 0)),
                pl.BlockSpec((chunk, V7X_LANES), lambda b, c: (c, 0)),
                const2((chunk, qk_w)), const2((chunk, qk_w)),
                const2((RET_HEADS, chunk, chunk)), const2((RET_HEADS, 1, RET_DV))]
    args = [p, cos_t, sin_t, qdec, kdec, dmat, cdec]
    state_spec = pl.BlockSpec((None, RET_HEADS, RET_DK, RET_DV), lambda b, c: (b, 0, 0, 0))
    if has_state:
        in_specs.append(state_spec)
        args.append(s0)
    y, s = pl.pallas_call(
        functools.partial(_ret_kernel, has_state=has_state),
        grid=(n_seq, n_chunks),
        in_specs=in_specs,
        out_specs=[pl.BlockSpec((chunk, RET_HEADS * RET_DV), lambda b, c: (b * n_chunks + c, 0)), state_spec],
        out_shape=[jax.ShapeDtypeStruct((n_seq * t_seq, RET_HEADS * RET_DV), BF16),
                   jax.ShapeDtypeStruct((n_seq, RET_HEADS, RET_DK, RET_DV), F32)],
        compiler_params=_cparams(2),
        name="retention",
    )(*args)
    return y, s


def _n_levels(chunk):
    return int(math.log2(chunk))


def _gla_constants(chunk):
    nl = _n_levels(chunk)
    i = np.arange(chunk)[:, None]
    t = np.arange(chunk)[None, :]
    groups, masks = [], []
    for l in range(nl):
        s = 1 << l
        base = (i // (2 * s)) * (2 * s)
        right = (i % (2 * s)) >= s
        m = np.where(right, (t >= base + s) & (t <= i), (t > i) & (t <= base + s - 1))
        groups.append(m)
        j = t
        masks.append(((i // (2 * s)) == (j // (2 * s))) & right & ((j % (2 * s)) < s))
    groups.append(t <= i)
    groups.append(t > i)
    masks.append(i == t)
    m_all = jnp.asarray(np.concatenate(groups, axis=0).astype(np.float32), BF16)
    mask_all = jnp.asarray(np.stack(masks).astype(np.float32))
    return m_all, mask_all


def _gla_kernel(*refs, has_state, chunk, n_valid):
    if has_state:
        p_ref, up_ref, bias_ref, nw_ref, mall_ref, mask_ref, s0_ref, y_ref, s_ref = refs
    else:
        p_ref, up_ref, bias_ref, nw_ref, mall_ref, mask_ref, y_ref, s_ref = refs
        s0_ref = None

    @pl.when(pl.program_id(1) == 0)
    def _():
        s_ref[...] = s0_ref[...] if has_state else jnp.zeros(s_ref.shape, F32)

    nl = _n_levels(chunk)
    qk_w = GLA_HEADS * GLA_DK
    v0 = 2 * qk_w
    g0 = v0 + GLA_HEADS * GLA_DV
    r0 = g0 + GLA_HEADS * GLA_DV
    x = _dot(p_ref[:, r0:r0 + V7X_LANES].astype(BF16), up_ref[...]) + bias_ref[...]
    la = (jnp.minimum(x, 0.0) - _log1p_exp_neg_abs(x)) * (1.0 / GLA_GATE_NORM)
    if n_valid < chunk:
        row = lax.broadcasted_iota(jnp.int32, la.shape, 0)
        la = jnp.where(row < n_valid, la, 0.0)
    g1, g2, g3 = _split3(la)
    e3 = _dot(mall_ref[...], jnp.concatenate([g1, g2, g3], axis=1))
    f_all = jnp.exp(e3[:, 0:qk_w] + e3[:, qk_w:2 * qk_w] + e3[:, 2 * qk_w:3 * qk_w])
    q = p_ref[:, 0:qk_w] * (GLA_DK ** -0.5)
    k = p_ref[:, qk_w:2 * qk_w]
    eye = (lax.broadcasted_iota(jnp.int32, (GLA_DK, GLA_DK), 0)
           == lax.broadcasted_iota(jnp.int32, (GLA_DK, GLA_DK), 1))
    for h in range(GLA_HEADS):
        hs = slice(h * GLA_DK, (h + 1) * GLA_DK)
        qh, kh = q[:, hs], k[:, hs]
        vh = p_ref[:, v0 + h * GLA_DV:v0 + (h + 1) * GLA_DV].astype(BF16)
        att = _dot_nt(qh.astype(BF16), kh.astype(BF16)) * mask_ref[nl]
        for l in range(nl):
            fl = f_all[l * chunk:(l + 1) * chunk, hs]
            att = att + _dot_nt((qh * fl).astype(BF16), (kh * fl).astype(BF16)) * mask_ref[l]
        fb = f_all[nl * chunk:(nl + 1) * chunk, hs]
        fs = f_all[(nl + 1) * chunk:(nl + 2) * chunk, hs]
        s_old = s_ref[h]
        o = _dot(att.astype(BF16), vh) + _dot((qh * fb).astype(BF16), s_old.astype(BF16))
        last = jnp.broadcast_to(fb[chunk - 1:chunk, :], (GLA_DK, GLA_DK))
        col = jnp.sum(jnp.where(eye, last, 0.0), axis=1, keepdims=True)
        s_ref[h] = s_old * col + _dot_tn((kh * fs).astype(BF16), vh)
        o = o * lax.rsqrt(jnp.mean(o * o, axis=-1, keepdims=True) + EPS) * nw_ref[...]
        g = p_ref[:, g0 + h * GLA_DV:g0 + (h + 1) * GLA_DV]
        y_ref[:, h * GLA_DV:(h + 1) * GLA_DV] = (_silu(g) * o).astype(y_ref.dtype)


def _gla_branch(p, s0, gk_up_pad, gk_bias, norm_w, *, n_seq, t_seq, chunk, n_valid):
    has_state = s0 is not None
    n_chunks = t_seq // chunk
    m_all, mask_all = _gla_constants(chunk)
    qk_w = GLA_HEADS * GLA_DK
    const2 = lambda shape: pl.BlockSpec(shape, lambda b, c: (0,) * len(shape))
    in_specs = [pl.BlockSpec((chunk, GLA_PROJ_PAD), lambda b, c: (b * n_chunks + c, 0)),
                const2((V7X_LANES, qk_w)), const2((1, qk_w)), const2((1, GLA_DV)),
                const2(m_all.shape), const2(mask_all.shape)]
    args = [p, gk_up_pad, gk_bias.reshape(1, qk_w), norm_w.reshape(1, GLA_DV), m_all, mask_all]
    state_spec = pl.BlockSpec((None, GLA_HEADS, GLA_DK, GLA_DV), lambda b, c: (b, 0, 0, 0))
    if has_state:
        in_specs.append(state_spec)
        args.append(s0)
    y, s = pl.pallas_call(
        functools.partial(_gla_kernel, has_state=has_state, chunk=chunk, n_valid=n_valid),
        grid=(n_seq, n_chunks),
        in_specs=in_specs,
        out_specs=[pl.BlockSpec((chunk, GLA_HEADS * GLA_DV), lambda b, c: (b * n_chunks + c, 0)), state_spec],
        out_shape=[jax.ShapeDtypeStruct((n_seq * t_seq, GLA_HEADS * GLA_DV), BF16),
                   jax.ShapeDtypeStruct((n_seq, GLA_HEADS, GLA_DK, GLA_DV), F32)],
        compiler_params=_cparams(2),
        name="gla",
    )(*args)
    return y, s


_RW_SRC = ((0, 1024), (1088, 2112), (2112, 3136), (1024, 1088), (3136, 3200), (3200, 3328))
_RW_DST = ((0, 1024), (3072, 3136), (1024, 3072), (3136, 3328))


def _rw_to_kernel(t):
    return jnp.concatenate([t[..., a:b] for a, b in _RW_SRC], axis=-1)


def _rw_from_kernel(t):
    return jnp.concatenate([t[..., a:b] for a, b in _RW_DST], axis=-1)


def _segsum(x, bd_ref):
    n = x.shape[1] // V7X_LANES
    rows = x.shape[0]
    stacked = jnp.concatenate([x[:, j * V7X_LANES:(j + 1) * V7X_LANES] for j in range(n)], axis=0)
    s = _dot3_right(stacked, bd_ref[...])
    return jnp.concatenate([s[j * rows:(j + 1) * rows, :] for j in range(n)], axis=1)


def _rwkv_kernel(*refs, has_state, chunk, n_valid):
    if has_state:
        (p_ref, mu_ref, w0_ref, a0_ref, kks_ref, ka_ref, rk_ref, lnw_ref, lnb_ref, wup_ref, aup_ref, gup_ref,
         tri_ref, bd_ref, prev0_ref, s0_ref, y_ref, s_ref, shift_ref) = refs
    else:
        (p_ref, mu_ref, w0_ref, a0_ref, kks_ref, ka_ref, rk_ref, lnw_ref, lnb_ref, wup_ref, aup_ref, gup_ref,
         tri_ref, bd_ref, y_ref, s_ref, shift_ref) = refs
        prev0_ref = s0_ref = None
    c_idx = pl.program_id(1)

    @pl.when(c_idx == 0)
    def _():
        s_ref[...] = s0_ref[...] if has_state else jnp.zeros(s_ref.shape, F32)
        shift_ref[...] = prev0_ref[...] if has_state else jnp.zeros(shift_ref.shape, F32)

    C = chunk
    W = RWKV_W
    p = p_ref[...]
    row = lax.broadcasted_iota(jnp.int32, p.shape, 0)
    prev = jnp.where(row == 0, shift_ref[...], pltpu.roll(p, 1, 0))
    shift_ref[...] = p[n_valid - 1:n_valid, :]
    xm = p + (prev - p) * mu_ref[...]
    r, k, v = xm[:, 0:W], xm[:, W:2 * W], xm[:, 2 * W:3 * W]
    o = 3 * W
    wd = xm[:, o:o + RWKV_DECAY_RANK]
    ad = xm[:, o + RWKV_DECAY_RANK:o + RWKV_DECAY_RANK + RWKV_A_RANK]
    gd = xm[:, o + RWKV_DECAY_RANK + RWKV_A_RANK:]
    z = w0_ref[...] + _dot(jnp.tanh(wd).astype(BF16), wup_ref[...])
    w_log = -(jnp.maximum(-z, 0.0) + _log1p_exp_neg_abs(z)) - 0.5
    lw = -jnp.exp(w_log)
    a = _sigmoid(a0_ref[...] + _dot(ad.astype(BF16), aup_ref[...]))
    g = _dot(_sigmoid(gd).astype(BF16), gup_ref[...])
    kk = k * kks_ref[...]
    kk = kk * lax.rsqrt(_segsum(kk * kk, bd_ref) + 1e-12)
    k2 = k * (1.0 + (a - 1.0) * ka_ref[...])
    if n_valid < C:
        valid = lax.broadcasted_iota(jnp.int32, lw.shape, 0) < n_valid
        lw = jnp.where(valid, lw, 0.0)
        k2 = jnp.where(valid, k2, 0.0)
        kk = jnp.where(valid, kk, 0.0)
    lp = _dot3_left(tri_ref[...], lw)
    lp_end = lp[C - 1:C, :]
    e_neg = jnp.exp(-lp)
    e_end = jnp.exp(lp_end - lp)
    beta = kk * a
    alpha_t = -kk * jnp.exp(lp - lw)
    r_t = r * jnp.exp(lp)
    beta_t, k_t = beta * e_neg, k2 * e_neg
    beta_e, k_e = beta * e_end, k2 * e_end
    dec_end = jnp.exp(lp_end)
    ri = lax.broadcasted_iota(jnp.int32, (C, C), 0)
    ci = lax.broadcasted_iota(jnp.int32, (C, C), 1)
    strict = (ri > ci).astype(F32)
    incl = (ri >= ci).astype(F32)
    eye = (ri == ci).astype(F32)
    incl2 = jnp.concatenate([incl, incl], axis=1)
    ys = []
    for h in range(RWKV_HEADS):
        hs = slice(h * RWKV_HD, (h + 1) * RWKV_HD)
        lhs = jnp.concatenate([alpha_t[:, hs], r_t[:, hs]], axis=0).astype(BF16)
        rhs = jnp.concatenate([beta_t[:, hs], k_t[:, hs]], axis=0).astype(BF16)
        m1 = _dot_nt(lhs, rhs)
        a_ab = m1[0:C, 0:C] * strict
        a_ak = m1[0:C, C:2 * C] * strict
        a_r = m1[C:2 * C, :] * incl2
        pw = a_ab
        tinv = eye + a_ab
        for _ in range(_n_levels(C) - 1):
            pwb = pw.astype(BF16)
            pw = _dot(pwb, pwb)
            tinv = tinv + _dot(pw.astype(BF16), tinv.astype(BF16))
        vh = v[:, hs]
        xv = _dot(a_ak.astype(BF16), vh.astype(BF16))
        wu = _dot(tinv.astype(BF16), jnp.concatenate([alpha_t[:, hs], xv], axis=1).astype(BF16))
        s_old = s_ref[h]
        gm = _dot_nt(jnp.concatenate([wu[:, 0:RWKV_HD], r_t[:, hs]], axis=0).astype(BF16), s_old.astype(BF16))
        u = gm[0:C, :] + wu[:, RWKV_HD:2 * RWKV_HD]
        uv = jnp.concatenate([u, vh], axis=0).astype(BF16)
        ys.append(gm[C:2 * C, :] + _dot(a_r.astype(BF16), uv))
        bk = jnp.concatenate([beta_e[:, hs], k_e[:, hs]], axis=0).astype(BF16)
        s_ref[h] = s_old * dec_end[:, hs] + _dot_tn(uv, bk)
    y = jnp.concatenate(ys, axis=1)
    mean = _segsum(y, bd_ref) * (1.0 / RWKV_HD)
    yc = y - mean
    var = _segsum(yc * yc, bd_ref) * (1.0 / RWKV_HD)
    yn = yc * lax.rsqrt(var + RWKV_GN_EPS) * lnw_ref[...] + lnb_ref[...]
    bonus = _segsum(r * k2 * rk_ref[...], bd_ref)
    y_ref[...] = ((yn + bonus * v) * g).astype(y_ref.dtype)


def _rwkv_branch(p, s0, prev0, prm, *, n_seq, t_seq, chunk, n_valid):
    has_state = s0 is not None
    n_chunks = t_seq // chunk
    tri = jnp.asarray(np.tril(np.ones((chunk, chunk), np.float32)), BF16)
    lane = np.arange(V7X_LANES)
    bd = jnp.asarray((lane[:, None] // RWKV_HD == lane[None, :] // RWKV_HD).astype(np.float32), BF16)
    const2 = lambda shape: pl.BlockSpec(shape, lambda b, c: (0,) * len(shape))
    row = lambda t: t.reshape(1, -1)
    vecs = [row(prm['mu']), row(prm['w0']), row(prm['a0']), row(prm['kk_scale']), row(prm['ka']), row(prm['rk']),
            row(prm['ln_w']), row(prm['ln_b'])]
    mats = [prm['w_up'], prm['a_up'], prm['g_up'], tri, bd]
    in_specs = ([pl.BlockSpec((chunk, RWKV_PROJ), lambda b, c: (b * n_chunks + c, 0))]
                + [const2(t.shape) for t in vecs] + [const2(t.shape) for t in mats])
    args = [p] + vecs + mats
    state_spec = pl.BlockSpec((None, RWKV_HEADS, RWKV_HD, RWKV_HD), lambda b, c: (b, 0, 0, 0))
    shift_spec = pl.BlockSpec((None, 1, RWKV_PROJ), lambda b, c: (b, 0, 0))
    if has_state:
        in_specs += [shift_spec, state_spec]
        args += [prev0.reshape(n_seq, 1, RWKV_PROJ), s0]
    y, s, shift = pl.pallas_call(
        functools.partial(_rwkv_kernel, has_state=has_state, chunk=chunk, n_valid=n_valid),
        grid=(n_seq, n_chunks),
        in_specs=in_specs,
        out_specs=[pl.BlockSpec((chunk, RWKV_W), lambda b, c: (b * n_chunks + c, 0)), state_spec, shift_spec],
        out_shape=[jax.ShapeDtypeStruct((n_seq * t_seq, RWKV_W), BF16),
                   jax.ShapeDtypeStruct((n_seq, RWKV_HEADS, RWKV_HD, RWKV_HD), F32),
                   jax.ShapeDtypeStruct((n_seq, 1, RWKV_PROJ), F32)],
        compiler_params=_cparams(2),
        name="rwkv7",
    )(*args)
    return y, s, shift.reshape(n_seq, RWKV_PROJ)


def _merge_kernel(x_ref, yr_ref, yg_ref, yw_ref, pm_ref, wb_ref, wo_ref, o_ref):
    D = D_MODEL
    merged = (_sigmoid(pm_ref[:, 0:D]) * _dot(yr_ref[...], wb_ref[0])
              + _sigmoid(pm_ref[:, D:2 * D]) * _dot(yg_ref[...], wb_ref[1])
              + _sigmoid(pm_ref[:, 2 * D:3 * D]) * _dot(yw_ref[...], wb_ref[2]))
    o_ref[...] = x_ref[...] + _dot(merged.astype(BF16), wo_ref[...])


def _merge(x, y_ret, y_gla, y_rw, p_merge, wb, wo, *, bm):
    m, d = x.shape
    tok = lambda w: pl.BlockSpec((bm, w), lambda i: (i, 0))
    return pl.pallas_call(
        _merge_kernel,
        grid=(m // bm,),
        in_specs=[tok(d), tok(d), tok(d), tok(d), tok(3 * d),
                  pl.BlockSpec((3, d, d), lambda i: (0, 0, 0)), pl.BlockSpec((d, d), lambda i: (0, 0))],
        out_specs=tok(d),
        out_shape=jax.ShapeDtypeStruct((m, d), F32),
        compiler_params=_cparams(1),
        name="merge",
    )(x, y_ret, y_gla, y_rw, p_merge, wb, wo)


def _ffn_kernel(x_ref, g_ref, w1_ref, w3_ref, w2_ref, o_ref, h_ref, acc_ref):
    f = pl.program_id(1)

    @pl.when(f == 0)
    def _():
        x = x_ref[...]
        ms = jnp.mean(x * x, axis=-1, keepdims=True)
        h_ref[...] = (x * lax.rsqrt(ms + EPS) * g_ref[...]).astype(BF16)
        acc_ref[...] = jnp.zeros(acc_ref.shape, F32)

    h = h_ref[...]
    t = (_silu(_dot(h, w1_ref[...])) * _dot(h, w3_ref[...])).astype(BF16)
    acc_ref[...] += _dot(t, w2_ref[...])

    @pl.when(f == pl.num_programs(1) - 1)
    def _():
        o_ref[...] = x_ref[...] + acc_ref[...]


def _ffn(x, g, w1, w3, w2, *, bm, bf):
    m, d = x.shape
    dff = w1.shape[1]
    return pl.pallas_call(
        _ffn_kernel,
        grid=(m // bm, dff // bf),
        in_specs=[pl.BlockSpec((bm, d), lambda i, f: (i, 0)), pl.BlockSpec((1, d), lambda i, f: (0, 0)),
                  pl.BlockSpec((d, bf), lambda i, f: (0, f)), pl.BlockSpec((d, bf), lambda i, f: (0, f)),
                  pl.BlockSpec((bf, d), lambda i, f: (f, 0))],
        out_specs=pl.BlockSpec((bm, d), lambda i, f: (i, 0)),
        out_shape=jax.ShapeDtypeStruct((m, d), F32),
        scratch_shapes=[pltpu.VMEM((bm, d), BF16), pltpu.VMEM((bm, d), F32)],
        compiler_params=_cparams(2),
        name="ffn_swiglu",
    )(x, g.reshape(1, d), w1, w3, w2)


def _moe_kernel(x_ref, g_ref, rt_ref, w1_ref, w3_ref, w2_ref, o_ref, h_ref, gate_ref, acc_ref):
    e = pl.program_id(1)
    f = pl.program_id(2)

    @pl.when((e == 0) & (f == 0))
    def _():
        x = x_ref[...]
        ms = jnp.mean(x * x, axis=-1, keepdims=True)
        hf = x * lax.rsqrt(ms + EPS) * g_ref[...]
        h_ref[...] = hf.astype(BF16)
        acc_ref[...] = jnp.zeros(acc_ref.shape, F32)
        h1, h2, h3 = _split3(hf)
        r1, r2, r3 = _split3(rt_ref[...])
        logits = (_dot(h1, r1) + (_dot(h1, r2) + _dot(h2, r1))
                  + (_dot(h1, r3) + _dot(h2, r2) + _dot(h3, r1)))
        lane = lax.broadcasted_iota(jnp.int32, logits.shape, 1).astype(F32)
        neg = jnp.float32(-jnp.inf)
        lg = jnp.where(lane < N_EXPERTS, logits, neg)
        m1 = jnp.max(lg, axis=-1, keepdims=True)
        i1 = jnp.min(jnp.where(lg == m1, lane, float(V7X_LANES)), axis=-1, keepdims=True)
        lg2 = jnp.where(lane == i1, neg, lg)
        m2 = jnp.max(lg2, axis=-1, keepdims=True)
        i2 = jnp.min(jnp.where(lg2 == m2, lane, float(V7X_LANES)), axis=-1, keepdims=True)
        ex = jnp.exp(m2 - m1)
        p1 = 1.0 / (1.0 + ex)
        gate_ref[...] = jnp.where(lane == i1, p1, 0.0) + jnp.where(lane == i2, ex * p1, 0.0)

    h = h_ref[...]
    t = (_silu(_dot(h, w1_ref[...])) * _dot(h, w3_ref[...])).astype(BF16)
    gate = gate_ref[...]
    lane = lax.broadcasted_iota(jnp.int32, gate.shape, 1)
    ge = jnp.sum(jnp.where(lane == e, gate, 0.0), axis=-1, keepdims=True)
    acc_ref[...] += ge * _dot(t, w2_ref[...])

    @pl.when((e == pl.num_programs(1) - 1) & (f == pl.num_programs(2) - 1))
    def _():
        o_ref[...] = x_ref[...] + acc_ref[...]


def _moe(x, g, router_pad, w1, w3, w2, *, bm, bf):
    m, d = x.shape
    n_e, _, dff = w1.shape
    return pl.pallas_call(
        _moe_kernel,
        grid=(m // bm, n_e, dff // bf),
        in_specs=[pl.BlockSpec((bm, d), lambda i, e, f: (i, 0)), pl.BlockSpec((1, d), lambda i, e, f: (0, 0)),
                  pl.BlockSpec((d, V7X_LANES), lambda i, e, f: (0, 0)),
                  pl.BlockSpec((None, d, bf), lambda i, e, f: (e, 0, f)),
                  pl.BlockSpec((None, d, bf), lambda i, e, f: (e, 0, f)),
                  pl.BlockSpec((None, bf, d), lambda i, e, f: (e, f, 0))],
        out_specs=pl.BlockSpec((bm, d), lambda i, e, f: (i, 0)),
        out_shape=jax.ShapeDtypeStruct((m, d), F32),
        scratch_shapes=[pltpu.VMEM((bm, d), BF16), pltpu.VMEM((bm, V7X_LANES), F32), pltpu.VMEM((bm, d), F32)],
        compiler_params=_cparams(3),
        name="moe_swiglu",
    )(x, g.reshape(1, d), router_pad, w1, w3, w2)


def _pad_time(t, n_seq, t_seq, t_pad):
    w = t.shape[1]
    t = t.reshape(n_seq, t_seq, w)
    t = jnp.pad(t, ((0, 0), (0, t_pad - t_seq), (0, 0)))
    return t.reshape(n_seq * t_pad, w)


def _unpad_time(t, n_seq, t_seq, t_pad):
    w = t.shape[1]
    return t.reshape(n_seq, t_pad, w)[:, :t_seq].reshape(n_seq * t_seq, w)


def kernel(x_prompt, x_sample, state_ret, state_gla, state_rwkv, state_shift, norm_mix, norm_ffn, w_in,
           gla_gk_up, gla_gk_bias, gla_norm_w, rwkv_mu, rwkv_w0, rwkv_w_up, rwkv_a0, rwkv_a_up, rwkv_g_up,
           rwkv_kk_scale, rwkv_ka, rwkv_rk, rwkv_ln_w, rwkv_ln_b, w_branch, w_out, ffn_w1, ffn_w3, ffn_w2,
           moe_router, moe_w1, moe_w3, moe_w2, norm_final):
    bp, tp, d = x_prompt.shape
    bs, ts, _ = x_sample.shape
    depth = w_in.shape[0]
    past_len = 16384
    n_p, n_s = bp * tp, bs * ts
    x = jnp.concatenate([x_prompt.reshape(n_p, d), x_sample.reshape(n_s, d)], axis=0)
    n_tok = n_p + n_s
    bm = 768 if n_tok % 768 == 0 else 512
    assert n_tok % bm == 0

    outs = {k: [] for k in ('ret_p', 'gla_p', 'rw_p', 'sh_p', 'ret_s', 'gla_s', 'rw_s', 'sh_s')}
    for l in range(depth):
        c0, c1, c2 = RET_PROJ, RET_PROJ + GLA_PROJ, RET_PROJ + GLA_PROJ + RWKV_PROJ
        w_l = w_in[l]
        w_ret = w_l[:, :c0].astype(BF16)
        w_gla = jnp.pad(w_l[:, c0:c1], ((0, 0), (0, GLA_PROJ_PAD - GLA_PROJ))).astype(BF16)
        w_rw = _rw_to_kernel(w_l[:, c1:c2]).astype(BF16)
        w_mg = w_l[:, c2:].astype(BF16)
        g_mix = norm_mix[l]
        p_ret = _norm_matmul(x, g_mix, w_ret, bm=bm, bn=1024)
        p_gla = _norm_matmul(x, g_mix, w_gla, bm=bm, bn=640)
        p_rw = _norm_matmul(x, g_mix, w_rw, bm=bm, bn=1664)
        p_mg = _norm_matmul(x, g_mix, w_mg, bm=bm, bn=1024)

        gk_up_pad = jnp.pad(gla_gk_up[l], ((0, V7X_LANES - GLA_GATE_RANK), (0, 0))).astype(BF16)
        rw_prm = dict(mu=_rw_to_kernel(rwkv_mu[l]), w0=rwkv_w0[l], a0=rwkv_a0[l], kk_scale=rwkv_kk_scale[l],
                      ka=rwkv_ka[l], rk=rwkv_rk[l], ln_w=rwkv_ln_w[l], ln_b=rwkv_ln_b[l],
                      w_up=rwkv_w_up[l].astype(BF16), a_up=rwkv_a_up[l].astype(BF16),
                      g_up=rwkv_g_up[l].astype(BF16))

        yr_p, sr_p = _ret_branch(p_ret, None, n_seq=bp, t_seq=tp, chunk=min(PROMPT_CHUNK_RET, tp),
                                 n_valid=min(PROMPT_CHUNK_RET, tp), pos0=0)
        yg_p, sg_p = _gla_branch(p_gla, None, gk_up_pad, gla_gk_bias[l], gla_norm_w[l], n_seq=bp, t_seq=tp,
                                 chunk=PROMPT_CHUNK, n_valid=PROMPT_CHUNK)
        yw_p, sw_p, sh_p = _rwkv_branch(p_rw, None, None, rw_prm, n_seq=bp, t_seq=tp,
                                        chunk=PROMPT_CHUNK, n_valid=PROMPT_CHUNK)

        cs = SAMPLE_CHUNK
        pad = lambda t: _pad_time(t[n_p:], bs, ts, cs)
        unpad = lambda t: _unpad_time(t, bs, ts, cs)
        yr_s, sr_s = _ret_branch(pad(p_ret), state_ret[l], n_seq=bs, t_seq=cs, chunk=cs, n_valid=ts, pos0=past_len)
        yg_s, sg_s = _gla_branch(pad(p_gla), state_gla[l], gk_up_pad, gla_gk_bias[l], gla_norm_w[l],
                                 n_seq=bs, t_seq=cs, chunk=cs, n_valid=ts)
        yw_s, sw_s, sh_s = _rwkv_branch(pad(p_rw), state_rwkv[l], _rw_to_kernel(state_shift[l]), rw_prm,
                                        n_seq=bs, t_seq=cs, chunk=cs, n_valid=ts)

        y_ret = jnp.concatenate([yr_p[:n_p], unpad(yr_s)], axis=0)
        y_gla = jnp.concatenate([yg_p[:n_p], unpad(yg_s)], axis=0)
        y_rw = jnp.concatenate([yw_p[:n_p], unpad(yw_s)], axis=0)
        x = _merge(x, y_ret, y_gla, y_rw, p_mg, w_branch[l].astype(BF16), w_out[l].astype(BF16), bm=bm)

        j = l // 2
        if l % 2 == 0:
            x = _ffn(x, norm_ffn[l], ffn_w1[j].astype(BF16), ffn_w3[j].astype(BF16), ffn_w2[j].astype(BF16),
                     bm=bm, bf=D_FF // 2)
        else:
            router_pad = jnp.pad(moe_router[j], ((0, 0), (0, V7X_LANES - N_EXPERTS)))
            x = _moe(x, norm_ffn[l], router_pad, moe_w1[j].astype(BF16), moe_w3[j].astype(BF16),
                     moe_w2[j].astype(BF16), bm=bm, bf=D_FF // 2)

        outs['ret_p'].append(sr_p); outs['gla_p'].append(sg_p); outs['rw_p'].append(sw_p)
        outs['sh_p'].append(_rw_from_kernel(sh_p))
        outs['ret_s'].append(sr_s); outs['gla_s'].append(sg_s); outs['rw_s'].append(sw_s)
        outs['sh_s'].append(_rw_from_kernel(sh_s))

    y = _rmsnorm(x, norm_final, bm=bm)
    st = lambda k: jnp.stack(outs[k])
    return (y[:n_p].reshape(bp, tp, d), y[n_p:].reshape(bs, ts, d),
            st('ret_p'), st('gla_p'), st('rw_p'), st('sh_p'),
            st('ret_s'), st('gla_s'), st('rw_s'), st('sh_s'))
```

```python
import functools
import math

import jax
import jax.numpy as jnp
import numpy as np
from jax import lax
from jax.experimental import pallas as pl
from jax.experimental.pallas import tpu as pltpu

F32 = jnp.float32
BF16 = jnp.bfloat16

V7X_LANES = 128
V7X_SUBLANES = 8
V7X_VMEM_LIMIT = 56 * 1024 * 1024

D_MODEL = 1024
RET_HEADS, RET_DK, RET_DV = 8, 64, 128
GLA_HEADS, GLA_DK, GLA_DV = 4, 128, 256
GLA_GATE_RANK, GLA_GATE_NORM = 16, 16.0
RWKV_HEADS, RWKV_HD = 16, 64
RWKV_W = RWKV_HEADS * RWKV_HD
RWKV_DECAY_RANK, RWKV_A_RANK, RWKV_GATE_RANK = 64, 64, 128
RWKV_GN_EPS = 64e-5
D_FF, N_EXPERTS = 2816, 8
ROPE_BASE = 10000.0
EPS = 1e-6
RET_PROJ = 2 * RET_HEADS * RET_DK + 2 * RET_HEADS * RET_DV
GLA_PROJ = 2 * GLA_HEADS * GLA_DK + 2 * GLA_HEADS * GLA_DV + GLA_GATE_RANK
GLA_PROJ_PAD = GLA_PROJ - GLA_GATE_RANK + V7X_LANES
RWKV_PROJ = 3 * RWKV_W + RWKV_DECAY_RANK + RWKV_A_RANK + RWKV_GATE_RANK
MERGE_COLS = 3 * D_MODEL

PROMPT_CHUNK_RET = 256
PROMPT_CHUNK = 64
SAMPLE_CHUNK = 8


def _cparams(n_grid):
    return pltpu.CompilerParams(dimension_semantics=("arbitrary",) * n_grid,
                                vmem_limit_bytes=V7X_VMEM_LIMIT)


def _dot(a, b):
    return jnp.dot(a, b, preferred_element_type=F32)


def _dot_nt(a, b):
    return lax.dot_general(a, b, (((1,), (1,)), ((), ())), preferred_element_type=F32)


def _dot_tn(a, b):
    return lax.dot_general(a, b, (((0,), (0,)), ((), ())), preferred_element_type=F32)


def _split3(x):
    x1 = x.astype(BF16)
    r1 = x - x1.astype(F32)
    x2 = r1.astype(BF16)
    x3 = (r1 - x2.astype(F32)).astype(BF16)
    return x1, x2, x3


def _dot3_left(m, x):
    x1, x2, x3 = _split3(x)
    return _dot(m, x1) + _dot(m, x2) + _dot(m, x3)


def _dot3_right(x, m):
    x1, x2, x3 = _split3(x)
    return _dot(x1, m) + _dot(x2, m) + _dot(x3, m)


def _sigmoid(x):
    return 1.0 / (1.0 + jnp.exp(-x))


def _silu(x):
    return x * _sigmoid(x)


def _log1p_exp_neg_abs(x):
    return jnp.log(1.0 + jnp.exp(-jnp.abs(x)))


def _norm_matmul_kernel(x_ref, g_ref, w_ref, o_ref, h_ref):
    @pl.when(pl.program_id(1) == 0)
    def _():
        x = x_ref[...]
        ms = jnp.mean(x * x, axis=-1, keepdims=True)
        h_ref[...] = (x * lax.rsqrt(ms + EPS) * g_ref[...]).astype(BF16)

    o_ref[...] = _dot(h_ref[...], w_ref[...]).astype(o_ref.dtype)


def _norm_matmul(x, g, w, *, bm, bn, out_dtype=F32):
    m, d = x.shape
    n = w.shape[1]
    assert m % bm == 0 and n % bn == 0
    return pl.pallas_call(
        _norm_matmul_kernel,
        grid=(m // bm, n // bn),
        in_specs=[pl.BlockSpec((bm, d), lambda i, j: (i, 0)),
                  pl.BlockSpec((1, d), lambda i, j: (0, 0)),
                  pl.BlockSpec((d, bn), lambda i, j: (0, j))],
        out_specs=pl.BlockSpec((bm, bn), lambda i, j: (i, j)),
        out_shape=jax.ShapeDtypeStruct((m, n), out_dtype),
        scratch_shapes=[pltpu.VMEM((bm, d), BF16)],
        compiler_params=_cparams(2),
        name="norm_matmul",
    )(x, g.reshape(1, d), w)


def _rmsnorm_kernel(x_ref, g_ref, o_ref):
    x = x_ref[...]
    ms = jnp.mean(x * x, axis=-1, keepdims=True)
    o_ref[...] = x * lax.rsqrt(ms + EPS) * g_ref[...]


def _rmsnorm(x, g, *, bm):
    m, d = x.shape
    return pl.pallas_call(
        _rmsnorm_kernel,
        grid=(m // bm,),
        in_specs=[pl.BlockSpec((bm, d), lambda i: (i, 0)), pl.BlockSpec((1, d), lambda i: (0, 0))],
        out_specs=pl.BlockSpec((bm, d), lambda i: (i, 0)),
        out_shape=jax.ShapeDtypeStruct((m, d), F32),
        compiler_params=_cparams(1),
        name="final_rmsnorm",
    )(x, g.reshape(1, d))


def _ret_kernel(*refs, has_state):
    if has_state:
        p_ref, cos_ref, sin_ref, qdec_ref, kdec_ref, dmat_ref, cdec_ref, s0_ref, y_ref, s_ref = refs
    else:
        p_ref, cos_ref, sin_ref, qdec_ref, kdec_ref, dmat_ref, cdec_ref, y_ref, s_ref = refs
        s0_ref = None

    @pl.when(pl.program_id(1) == 0)
    def _():
        s_ref[...] = s0_ref[...] if has_state else jnp.zeros(s_ref.shape, F32)

    qk_w = RET_HEADS * RET_DK
    reps = qk_w // V7X_LANES
    cos = jnp.concatenate([cos_ref[...]] * reps, axis=1)
    sin = jnp.concatenate([sin_ref[...]] * reps, axis=1)
    lane = lax.broadcasted_iota(jnp.int32, cos.shape, 1)
    first_half = (lane & (RET_DK - 1)) < (RET_DK // 2)

    def rope(x):
        swapped = jnp.where(first_half, pltpu.roll(x, qk_w - RET_DK // 2, 1), pltpu.roll(x, RET_DK // 2, 1))
        return x * cos + swapped * sin

    qr = rope(p_ref[:, 0:qk_w])
    kr = rope(p_ref[:, qk_w:2 * qk_w]) * (RET_DK ** -0.5)
    qb, kb = qr.astype(BF16), kr.astype(BF16)
    qd = (qr * qdec_ref[...]).astype(BF16)
    kd = (kr * kdec_ref[...]).astype(BF16)
    v0 = 2 * qk_w
    g0 = v0 + RET_HEADS * RET_DV
    for h in range(RET_HEADS):
        sl = slice(h * RET_DK, (h + 1) * RET_DK)
        vs = slice(h * RET_DV, (h + 1) * RET_DV)
        v = p_ref[:, v0 + h * RET_DV:v0 + (h + 1) * RET_DV].astype(BF16)
        sc = _dot_nt(qb[:, sl], kb[:, sl]) * dmat_ref[h]
        s_old = s_ref[h]
        o = _dot(sc.astype(BF16), v) + _dot(qd[:, sl], s_old.astype(BF16))
        s_ref[h] = s_old * cdec_ref[h] + _dot_tn(kd[:, sl], v)
        o = o * lax.rsqrt(jnp.mean(o * o, axis=-1, keepdims=True) + EPS)
        g = p_ref[:, g0 + h * RET_DV:g0 + (h + 1) * RET_DV]
        y_ref[:, vs] = (_silu(g) * o).astype(y_ref.dtype)


def _ret_tables(chunk, n_valid, t_total, pos0):
    log_gamma = jnp.log1p(-jnp.exp2(-5.0 - jnp.arange(RET_HEADS, dtype=F32)))
    i = jnp.arange(chunk, dtype=F32)
    diff = i[:, None] - i[None, :]
    dmat = jnp.where(diff >= 0, jnp.exp(log_gamma[:, None, None] * jnp.maximum(diff, 0.0)), 0.0)
    q_dec = jnp.exp(log_gamma[:, None] * (i + 1.0))
    k_dec = jnp.where(i < n_valid, jnp.exp(log_gamma[:, None] * jnp.maximum(n_valid - 1.0 - i, 0.0)), 0.0)
    c_dec = jnp.exp(log_gamma * n_valid)
    spread = lambda t: jnp.repeat(t.T, RET_DK, axis=1)
    half = RET_DK // 2
    inv = ROPE_BASE ** (-jnp.arange(half, dtype=F32) / half)
    pos = (pos0 + jnp.arange(t_total, dtype=jnp.int32)).astype(F32)
    ang = pos[:, None] * inv[None, :]
    cos, sin = jnp.cos(ang), jnp.sin(ang)
    per_lane = V7X_LANES // RET_DK
    cos_t = jnp.tile(jnp.concatenate([cos, cos], axis=1), (1, per_lane))
    sin_t = jnp.tile(jnp.concatenate([-sin, sin], axis=1), (1, per_lane))
    cdec = jnp.broadcast_to(c_dec[:, None, None], (RET_HEADS, 1, RET_DV))
    return cos_t, sin_t, spread(q_dec), spread(k_dec), dmat, cdec


def _ret_branch(p, s0, *, n_seq, t_seq, chunk, n_valid, pos0):
    has_state = s0 is not None
    n_chunks = t_seq // chunk
    cos_t, sin_t, qdec, kdec, dmat, cdec = _ret_tables(chunk, n_valid, t_seq, pos0)
    qk_w = RET_HEADS * RET_DK
    const2 = lambda shape: pl.BlockSpec(shape, lambda b, c: (0,) * len(shape))
    in_specs = [pl.BlockSpec((chunk, RET_PROJ), lambda b, c: (b * n_chunks + c, 0)),
                pl.BlockSpec((chunk, V7X_LANES), lambda b, c: (c, 0)),
                pl.BlockSpec((chunk, V7X_LANES), lambda b, c: (c, 0)),
                const2((chunk, qk_w)), const2((chunk, qk_w)),
                const2((RET_HEADS, chunk, chunk)), const2((RET_HEADS, 1, RET_DV))]
    args = [p, cos_t, sin_t, qdec, kdec, dmat, cdec]
    state_spec = pl.BlockSpec((None, RET_HEADS, RET_DK, RET_DV), lambda b, c: (b, 0, 0, 0))
    if has_state:
        in_specs.append(state_spec)
        args.append(s0)
    y, s = pl.pallas_call(
        functools.partial(_ret_kernel, has_state=has_state),
        grid=(n_seq, n_chunks),
        in_specs=in_specs,
        out_specs=[pl.BlockSpec((chunk, RET_HEADS * RET_DV), lambda b, c: (b * n_chunks + c, 0)), state_spec],
        out_shape=[jax.ShapeDtypeStruct((n_seq * t_seq, RET_HEADS * RET_DV), BF16),
                   jax.ShapeDtypeStruct((n_seq, RET_HEADS, RET_DK, RET_DV), F32)],
        compiler_params=_cparams(2),
        name="retention",
    )(*args)
    return y, s


def _n_levels(chunk):
    return int(math.log2(chunk))


def _gla_constants(chunk):
    nl = _n_levels(chunk)
    i = np.arange(chunk)[:, None]
    t = np.arange(chunk)[None, :]
    groups, masks = [], []
    for l in range(nl):
        s = 1 << l
        base = (i // (2 * s)) * (2 * s)
        right = (i % (2 * s)) >= s
        m = np.where(right, (t >= base + s) & (t <= i), (t > i) & (t <= base + s - 1))
        groups.append(m)
        j = t
        masks.append(((i // (2 * s)) == (j // (2 * s))) & right & ((j % (2 * s)) < s))
    groups.append(t <= i)
    groups.append(t > i)
    masks.append(i == t)
    m_all = jnp.asarray(np.concatenate(groups, axis=0).astype(np.float32), BF16)
    mask_all = jnp.asarray(np.stack(masks).astype(np.float32))
    return m_all, mask_all


def _gla_kernel(*refs, has_state, chunk, n_valid):
    if has_state:
        p_ref, up_ref, bias_ref, nw_ref, mall_ref, mask_ref, s0_ref, y_ref, s_ref = refs
    else:
        p_ref, up_ref, bias_ref, nw_ref, mall_ref, mask_ref, y_ref, s_ref = refs
        s0_ref = None

    @pl.when(pl.program_id(1) == 0)
    def _():
        s_ref[...] = s0_ref[...] if has_state else jnp.zeros(s_ref.shape, F32)

    nl = _n_levels(chunk)
    qk_w = GLA_HEADS * GLA_DK
    v0 = 2 * qk_w
    g0 = v0 + GLA_HEADS * GLA_DV
    r0 = g0 + GLA_HEADS * GLA_DV
    x = _dot(p_ref[:, r0:r0 + V7X_LANES].astype(BF16), up_ref[...]) + bias_ref[...]
    la = (jnp.minimum(x, 0.0) - _log1p_exp_neg_abs(x)) * (1.0 / GLA_GATE_NORM)
    if n_valid < chunk:
        row = lax.broadcasted_iota(jnp.int32, la.shape, 0)
        la = jnp.where(row < n_valid, la, 0.0)
    g1, g2, g3 = _split3(la)
    e3 = _dot(mall_ref[...], jnp.concatenate([g1, g2, g3], axis=1))
    f_all = jnp.exp(e3[:, 0:qk_w] + e3[:, qk_w:2 * qk_w] + e3[:, 2 * qk_w:3 * qk_w])
    q = p_ref[:, 0:qk_w] * (GLA_DK ** -0.5)
    k = p_ref[:, qk_w:2 * qk_w]
    eye = (lax.broadcasted_iota(jnp.int32, (GLA_DK, GLA_DK), 0)
           == lax.broadcasted_iota(jnp.int32, (GLA_DK, GLA_DK), 1))
    H = range(GLA_HEADS)
    hs = [slice(h * GLA_DK, (h + 1) * GLA_DK) for h in H]
    vh = [p_ref[:, v0 + h * GLA_DV:v0 + (h + 1) * GLA_DV].astype(BF16) for h in H]
    qb, kb = q.astype(BF16), k.astype(BF16)
    att = [_dot_nt(qb[:, hs[h]], kb[:, hs[h]]) * mask_ref[nl] for h in H]
    for l in range(nl):
        fl = f_all[l * chunk:(l + 1) * chunk, :]
        ql, kl = (q * fl).astype(BF16), (k * fl).astype(BF16)
        att = [att[h] + _dot_nt(ql[:, hs[h]], kl[:, hs[h]]) * mask_ref[l] for h in H]
    fb = f_all[nl * chunk:(nl + 1) * chunk, :]
    fs = f_all[(nl + 1) * chunk:(nl + 2) * chunk, :]
    qf, kf = (q * fb).astype(BF16), (k * fs).astype(BF16)
    s_old = [s_ref[h] for h in H]
    o = [_dot(att[h].astype(BF16), vh[h]) + _dot(qf[:, hs[h]], s_old[h].astype(BF16)) for h in H]
    for h in H:
        last = jnp.broadcast_to(fb[chunk - 1:chunk, hs[h]], (GLA_DK, GLA_DK))
        col = jnp.sum(jnp.where(eye, last, 0.0), axis=1, keepdims=True)
        s_ref[h] = s_old[h] * col + _dot_tn(kf[:, hs[h]], vh[h])
    for h in H:
        on = o[h] * lax.rsqrt(jnp.mean(o[h] * o[h], axis=-1, keepdims=True) + EPS) * nw_ref[...]
        g = p_ref[:, g0 + h * GLA_DV:g0 + (h + 1) * GLA_DV]
        y_ref[:, h * GLA_DV:(h + 1) * GLA_DV] = (_silu(g) * on).astype(y_ref.dtype)


def _gla_branch(p, s0, gk_up_pad, gk_bias, norm_w, *, n_seq, t_seq, chunk, n_valid):
    has_state = s0 is not None
    n_chunks = t_seq // chunk
    m_all, mask_all = _gla_constants(chunk)
    qk_w = GLA_HEADS * GLA_DK
    const2 = lambda shape: pl.BlockSpec(shape, lambda b, c: (0,) * len(shape))
    in_specs = [pl.BlockSpec((chunk, GLA_PROJ_PAD), lambda b, c: (b * n_chunks + c, 0)),
                const2((V7X_LANES, qk_w)), const2((1, qk_w)), const2((1, GLA_DV)),
                const2(m_all.shape), const2(mask_all.shape)]
    args = [p, gk_up_pad, gk_bias.reshape(1, qk_w), norm_w.reshape(1, GLA_DV), m_all, mask_all]
    state_spec = pl.BlockSpec((None, GLA_HEADS, GLA_DK, GLA_DV), lambda b, c: (b, 0, 0, 0))
    if has_state:
        in_specs.append(state_spec)
        args.append(s0)
    y, s = pl.pallas_call(
        functools.partial(_gla_kernel, has_state=has_state, chunk=chunk, n_valid=n_valid),
        grid=(n_seq, n_chunks),
        in_specs=in_specs,
        out_specs=[pl.BlockSpec((chunk, GLA_HEADS * GLA_DV), lambda b, c: (b * n_chunks + c, 0)), state_spec],
        out_shape=[jax.ShapeDtypeStruct((n_seq * t_seq, GLA_HEADS * GLA_DV), BF16),
                   jax.ShapeDtypeStruct((n_seq, GLA_HEADS, GLA_DK, GLA_DV), F32)],
        compiler_params=_cparams(2),
        name="gla",
    )(*args)
    return y, s


_RW_SRC = ((0, 1024), (1088, 2112), (2112, 3136), (1024, 1088), (3136, 3200), (3200, 3328))
_RW_DST = ((0, 1024), (3072, 3136), (1024, 3072), (3136, 3328))


def _rw_to_kernel(t):
    return jnp.concatenate([t[..., a:b] for a, b in _RW_SRC], axis=-1)


def _rw_from_kernel(t):
    return jnp.concatenate([t[..., a:b] for a, b in _RW_DST], axis=-1)


def _segsum(x, bd_ref):
    n = x.shape[1] // V7X_LANES
    rows = x.shape[0]
    stacked = jnp.concatenate([x[:, j * V7X_LANES:(j + 1) * V7X_LANES] for j in range(n)], axis=0)
    s = _dot3_right(stacked, bd_ref[...])
    return jnp.concatenate([s[j * rows:(j + 1) * rows, :] for j in range(n)], axis=1)


def _rwkv_kernel(*refs, has_state, chunk, n_valid):
    if has_state:
        (p_ref, mu_ref, w0_ref, a0_ref, kks_ref, ka_ref, rk_ref, lnw_ref, lnb_ref, wup_ref, aup_ref, gup_ref,
         tri_ref, bd_ref, prev0_ref, s0_ref, y_ref, s_ref, shift_ref) = refs
    else:
        (p_ref, mu_ref, w0_ref, a0_ref, kks_ref, ka_ref, rk_ref, lnw_ref, lnb_ref, wup_ref, aup_ref, gup_ref,
         tri_ref, bd_ref, y_ref, s_ref, shift_ref) = refs
        prev0_ref = s0_ref = None
    c_idx = pl.program_id(1)

    @pl.when(c_idx == 0)
    def _():
        s_ref[...] = s0_ref[...] if has_state else jnp.zeros(s_ref.shape, F32)
        shift_ref[...] = prev0_ref[...] if has_state else jnp.zeros(shift_ref.shape, F32)

    C = chunk
    W = RWKV_W
    p = p_ref[...]
    row = lax.broadcasted_iota(jnp.int32, p.shape, 0)
    prev = jnp.where(row == 0, shift_ref[...], pltpu.roll(p, 1, 0))
    shift_ref[...] = p[n_valid - 1:n_valid, :]
    xm = p + (prev - p) * mu_ref[...]
    r, k, v = xm[:, 0:W], xm[:, W:2 * W], xm[:, 2 * W:3 * W]
    o = 3 * W
    wd = xm[:, o:o + RWKV_DECAY_RANK]
    ad = xm[:, o + RWKV_DECAY_RANK:o + RWKV_DECAY_RANK + RWKV_A_RANK]
    gd = xm[:, o + RWKV_DECAY_RANK + RWKV_A_RANK:]
    z = w0_ref[...] + _dot(jnp.tanh(wd).astype(BF16), wup_ref[...])
    w_log = -(jnp.maximum(-z, 0.0) + _log1p_exp_neg_abs(z)) - 0.5
    lw = -jnp.exp(w_log)
    a = _sigmoid(a0_ref[...] + _dot(ad.astype(BF16), aup_ref[...]))
    g = _dot(_sigmoid(gd).astype(BF16), gup_ref[...])
    kk = k * kks_ref[...]
    kk = kk * lax.rsqrt(_segsum(kk * kk, bd_ref) + 1e-12)
    k2 = k * (1.0 + (a - 1.0) * ka_ref[...])
    if n_valid < C:
        valid = lax.broadcasted_iota(jnp.int32, lw.shape, 0) < n_valid
        lw = jnp.where(valid, lw, 0.0)
        k2 = jnp.where(valid, k2, 0.0)
        kk = jnp.where(valid, kk, 0.0)
    lp = _dot3_left(tri_ref[...], lw)
    lp_end = lp[C - 1:C, :]
    e_neg = jnp.exp(-lp)
    e_end = jnp.exp(lp_end - lp)
    beta = kk * a
    alpha_t = -kk * jnp.exp(lp - lw)
    r_t = r * jnp.exp(lp)
    beta_t, k_t = beta * e_neg, k2 * e_neg
    beta_e, k_e = beta * e_end, k2 * e_end
    dec_end = jnp.exp(lp_end)
    ri = lax.broadcasted_iota(jnp.int32, (C, C), 0)
    ci = lax.broadcasted_iota(jnp.int32, (C, C), 1)
    strict = ri > ci
    eye = (ri == ci).astype(F32)
    ri2 = lax.broadcasted_iota(jnp.int32, (C, 2 * C), 0)
    ci2 = lax.broadcasted_iota(jnp.int32, (C, 2 * C), 1)
    incl2 = ri2 >= (ci2 & (C - 1))
    H = range(RWKV_HEADS)
    hs = [slice(h * RWKV_HD, (h + 1) * RWKV_HD) for h in H]
    cat0 = lambda a, b: jnp.concatenate([a, b], axis=0)
    m1 = [_dot_nt(cat0(alpha_t[:, hs[h]], r_t[:, hs[h]]).astype(BF16),
                  cat0(beta_t[:, hs[h]], k_t[:, hs[h]]).astype(BF16)) for h in H]
    a_ab = [jnp.where(strict, m1[h][0:C, 0:C], 0.0) for h in H]
    xv = [_dot(jnp.where(strict, m1[h][0:C, C:2 * C], 0.0).astype(BF16), v[:, hs[h]].astype(BF16)) for h in H]
    pw = a_ab
    tinv = [eye + a_ab[h] for h in H]
    for _ in range(_n_levels(C) - 1):
        pwb = [pw[h].astype(BF16) for h in H]
        pw = [_dot(pwb[h], pwb[h]) for h in H]
        tinv = [tinv[h] + _dot(pw[h].astype(BF16), tinv[h].astype(BF16)) for h in H]
    wu = [_dot(tinv[h].astype(BF16), jnp.concatenate([alpha_t[:, hs[h]], xv[h]], axis=1).astype(BF16))
          for h in H]
    s_old = [s_ref[h] for h in H]
    gm = [_dot_nt(cat0(wu[h][:, 0:RWKV_HD], r_t[:, hs[h]]).astype(BF16), s_old[h].astype(BF16)) for h in H]
    uv = [cat0(gm[h][0:C, :] + wu[h][:, RWKV_HD:2 * RWKV_HD], v[:, hs[h]]).astype(BF16) for h in H]
    ys = [gm[h][C:2 * C, :] + _dot(jnp.where(incl2, m1[h][C:2 * C, :], 0.0).astype(BF16), uv[h]) for h in H]
    for h in H:
        bk = cat0(beta_e[:, hs[h]], k_e[:, hs[h]]).astype(BF16)
        s_ref[h] = s_old[h] * dec_end[:, hs[h]] + _dot_tn(uv[h], bk)
    y = jnp.concatenate(ys, axis=1)
    mean = _segsum(y, bd_ref) * (1.0 / RWKV_HD)
    yc = y - mean
    var = _segsum(yc * yc, bd_ref) * (1.0 / RWKV_HD)
    yn = yc * lax.rsqrt(var + RWKV_GN_EPS) * lnw_ref[...] + lnb_ref[...]
    bonus = _segsum(r * k2 * rk_ref[...], bd_ref)
    y_ref[...] = ((yn + bonus * v) * g).astype(y_ref.dtype)


def _rwkv_branch(p, s0, prev0, prm, *, n_seq, t_seq, chunk, n_valid):
    has_state = s0 is not None
    n_chunks = t_seq // chunk
    tri = jnp.asarray(np.tril(np.ones((chunk, chunk), np.float32)), BF16)
    lane = np.arange(V7X_LANES)
    bd = jnp.asarray((lane[:, None] // RWKV_HD == lane[None, :] // RWKV_HD).astype(np.float32), BF16)
    const2 = lambda shape: pl.BlockSpec(shape, lambda b, c: (0,) * len(shape))
    row = lambda t: t.reshape(1, -1)
    vecs = [row(prm['mu']), row(prm['w0']), row(prm['a0']), row(prm['kk_scale']), row(prm['ka']), row(prm['rk']),
            row(prm['ln_w']), row(prm['ln_b'])]
    mats = [prm['w_up'], prm['a_up'], prm['g_up'], tri, bd]
    in_specs = ([pl.BlockSpec((chunk, RWKV_PROJ), lambda b, c: (b * n_chunks + c, 0))]
                + [const2(t.shape) for t in vecs] + [const2(t.shape) for t in mats])
    args = [p] + vecs + mats
    state_spec = pl.BlockSpec((None, RWKV_HEADS, RWKV_HD, RWKV_HD), lambda b, c: (b, 0, 0, 0))
    shift_spec = pl.BlockSpec((None, 1, RWKV_PROJ), lambda b, c: (b, 0, 0))
    if has_state:
        in_specs += [shift_spec, state_spec]
        args += [prev0.reshape(n_seq, 1, RWKV_PROJ), s0]
    y, s, shift = pl.pallas_call(
        functools.partial(_rwkv_kernel, has_state=has_state, chunk=chunk, n_valid=n_valid),
        grid=(n_seq, n_chunks),
        in_specs=in_specs,
        out_specs=[pl.BlockSpec((chunk, RWKV_W), lambda b, c: (b * n_chunks + c, 0)), state_spec, shift_spec],
        out_shape=[jax.ShapeDtypeStruct((n_seq * t_seq, RWKV_W), BF16),
                   jax.ShapeDtypeStruct((n_seq, RWKV_HEADS, RWKV_HD, RWKV_HD), F32),
                   jax.ShapeDtypeStruct((n_seq, 1, RWKV_PROJ), F32)],
        compiler_params=_cparams(2),
        name="rwkv7",
    )(*args)
    return y, s, shift.reshape(n_seq, RWKV_PROJ)


def _merge_kernel(x_ref, yr_ref, yg_ref, yw_ref, pm_ref, wb_ref, wo_ref, o_ref):
    D = D_MODEL
    merged = (_sigmoid(pm_ref[:, 0:D]) * _dot(yr_ref[...], wb_ref[0])
              + _sigmoid(pm_ref[:, D:2 * D]) * _dot(yg_ref[...], wb_ref[1])
              + _sigmoid(pm_ref[:, 2 * D:3 * D]) * _dot(yw_ref[...], wb_ref[2]))
    o_ref[...] = x_ref[...] + _dot(merged.astype(BF16), wo_ref[...])


def _merge(x, y_ret, y_gla, y_rw, p_merge, wb, wo, *, bm):
    m, d = x.shape
    tok = lambda w: pl.BlockSpec((bm, w), lambda i: (i, 0))
    return pl.pallas_call(
        _merge_kernel,
        grid=(m // bm,),
        in_specs=[tok(d), tok(d), tok(d), tok(d), tok(3 * d),
                  pl.BlockSpec((3, d, d), lambda i: (0, 0, 0)), pl.BlockSpec((d, d), lambda i: (0, 0))],
        out_specs=tok(d),
        out_shape=jax.ShapeDtypeStruct((m, d), F32),
        compiler_params=_cparams(1),
        name="merge",
    )(x, y_ret, y_gla, y_rw, p_merge, wb, wo)


def _ffn_kernel(x_ref, g_ref, w1_ref, w3_ref, w2_ref, o_ref, h_ref, acc_ref):
    f = pl.program_id(1)

    @pl.when(f == 0)
    def _():
        x = x_ref[...]
        ms = jnp.mean(x * x, axis=-1, keepdims=True)
        h_ref[...] = (x * lax.rsqrt(ms + EPS) * g_ref[...]).astype(BF16)
        acc_ref[...] = jnp.zeros(acc_ref.shape, F32)

    h = h_ref[...]
    t = (_silu(_dot(h, w1_ref[...])) * _dot(h, w3_ref[...])).astype(BF16)
    acc_ref[...] += _dot(t, w2_ref[...])

    @pl.when(f == pl.num_programs(1) - 1)
    def _():
        o_ref[...] = x_ref[...] + acc_ref[...]


def _ffn(x, g, w1, w3, w2, *, bm, bf):
    m, d = x.shape
    dff = w1.shape[1]
    return pl.pallas_call(
        _ffn_kernel,
        grid=(m // bm, dff // bf),
        in_specs=[pl.BlockSpec((bm, d), lambda i, f: (i, 0)), pl.BlockSpec((1, d), lambda i, f: (0, 0)),
                  pl.BlockSpec((d, bf), lambda i, f: (0, f)), pl.BlockSpec((d, bf), lambda i, f: (0, f)),
                  pl.BlockSpec((bf, d), lambda i, f: (f, 0))],
        out_specs=pl.BlockSpec((bm, d), lambda i, f: (i, 0)),
        out_shape=jax.ShapeDtypeStruct((m, d), F32),
        scratch_shapes=[pltpu.VMEM((bm, d), BF16), pltpu.VMEM((bm, d), F32)],
        compiler_params=_cparams(2),
        name="ffn_swiglu",
    )(x, g.reshape(1, d), w1, w3, w2)


def _moe_kernel(x_ref, g_ref, rt_ref, w1_ref, w3_ref, w2_ref, o_ref, h_ref, gate_ref, acc_ref):
    e = pl.program_id(1)
    f = pl.program_id(2)

    @pl.when((e == 0) & (f == 0))
    def _():
        x = x_ref[...]
        ms = jnp.mean(x * x, axis=-1, keepdims=True)
        hf = x * lax.rsqrt(ms + EPS) * g_ref[...]
        h_ref[...] = hf.astype(BF16)
        acc_ref[...] = jnp.zeros(acc_ref.shape, F32)
        h1, h2, h3 = _split3(hf)
        r1, r2, r3 = _split3(rt_ref[...])
        logits = (_dot(h1, r1) + (_dot(h1, r2) + _dot(h2, r1))
                  + (_dot(h1, r3) + _dot(h2, r2) + _dot(h3, r1)))
        lane = lax.broadcasted_iota(jnp.int32, logits.shape, 1).astype(F32)
        neg = jnp.float32(-jnp.inf)
        lg = jnp.where(lane < N_EXPERTS, logits, neg)
        m1 = jnp.max(lg, axis=-1, keepdims=True)
        i1 = jnp.min(jnp.where(lg == m1, lane, float(V7X_LANES)), axis=-1, keepdims=True)
        lg2 = jnp.where(lane == i1, neg, lg)
        m2 = jnp.max(lg2, axis=-1, keepdims=True)
        i2 = jnp.min(jnp.where(lg2 == m2, lane, float(V7X_LANES)), axis=-1, keepdims=True)
        ex = jnp.exp(m2 - m1)
        p1 = 1.0 / (1.0 + ex)
        gate_ref[...] = jnp.where(lane == i1, p1, 0.0) + jnp.where(lane == i2, ex * p1, 0.0)

    h = h_ref[...]
    t = (_silu(_dot(h, w1_ref[...])) * _dot(h, w3_ref[...])).astype(BF16)
    gate = gate_ref[...]
    lane = lax.broadcasted_iota(jnp.int32, gate.shape, 1)
    ge = jnp.sum(jnp.where(lane == e, gate, 0.0), axis=-1, keepdims=True)
    acc_ref[...] += ge * _dot(t, w2_ref[...])

    @pl.when((e == pl.num_programs(1) - 1) & (f == pl.num_programs(2) - 1))
    def _():
        o_ref[...] = x_ref[...] + acc_ref[...]


def _moe(x, g, router_pad, w1, w3, w2, *, bm, bf):
    m, d = x.shape
    n_e, _, dff = w1.shape
    return pl.pallas_call(
        _moe_kernel,
        grid=(m // bm, n_e, dff // bf),
        in_specs=[pl.BlockSpec((bm, d), lambda i, e, f: (i, 0)), pl.BlockSpec((1, d), lambda i, e, f: (0, 0)),
                  pl.BlockSpec((d, V7X_LANES), lambda i, e, f: (0, 0)),
                  pl.BlockSpec((None, d, bf), lambda i, e, f: (e, 0, f)),
                  pl.BlockSpec((None, d, bf), lambda i, e, f: (e, 0, f)),
                  pl.BlockSpec((None, bf, d), lambda i, e, f: (e, f, 0))],
        out_specs=pl.BlockSpec((bm, d), lambda i, e, f: (i, 0)),
        out_shape=jax.ShapeDtypeStruct((m, d), F32),
        scratch_shapes=[pltpu.VMEM((bm, d), BF16), pltpu.VMEM((bm, V7X_LANES), F32), pltpu.VMEM((bm, d), F32)],
        compiler_params=_cparams(3),
        name="moe_swiglu",
    )(x, g.reshape(1, d), router_pad, w1, w3, w2)


def _pad_time(t, n_seq, t_seq, t_pad):
    w = t.shape[1]
    t = t.reshape(n_seq, t_seq, w)
    t = jnp.pad(t, ((0, 0), (0, t_pad - t_seq), (0, 0)))
    return t.reshape(n_seq * t_pad, w)


def _unpad_time(t, n_seq, t_seq, t_pad):
    w = t.shape[1]
    return t.reshape(n_seq, t_pad, w)[:, :t_seq].reshape(n_seq * t_seq, w)


def kernel(x_prompt, x_sample, state_ret, state_gla, state_rwkv, state_shift, norm_mix, norm_ffn, w_in,
           gla_gk_up, gla_gk_bias, gla_norm_w, rwkv_mu, rwkv_w0, rwkv_w_up, rwkv_a0, rwkv_a_up, rwkv_g_up,
           rwkv_kk_scale, rwkv_ka, rwkv_rk, rwkv_ln_w, rwkv_ln_b, w_branch, w_out, ffn_w1, ffn_w3, ffn_w2,
           moe_router, moe_w1, moe_w3, moe_w2, norm_final):
    bp, tp, d = x_prompt.shape
    bs, ts, _ = x_sample.shape
    depth = w_in.shape[0]
    past_len = 16384
    n_p, n_s = bp * tp, bs * ts
    x = jnp.concatenate([x_prompt.reshape(n_p, d), x_sample.reshape(n_s, d)], axis=0)
    n_tok = n_p + n_s
    bm = 768 if n_tok % 768 == 0 else 512
    assert n_tok % bm == 0

    outs = {k: [] for k in ('ret_p', 'gla_p', 'rw_p', 'sh_p', 'ret_s', 'gla_s', 'rw_s', 'sh_s')}
    for l in range(depth):
        c0, c1, c2 = RET_PROJ, RET_PROJ + GLA_PROJ, RET_PROJ + GLA_PROJ + RWKV_PROJ
        w_l = w_in[l]
        w_ret = w_l[:, :c0].astype(BF16)
        w_gla = jnp.pad(w_l[:, c0:c1], ((0, 0), (0, GLA_PROJ_PAD - GLA_PROJ))).astype(BF16)
        w_rw = _rw_to_kernel(w_l[:, c1:c2]).astype(BF16)
        w_mg = w_l[:, c2:].astype(BF16)
        g_mix = norm_mix[l]
        p_ret = _norm_matmul(x, g_mix, w_ret, bm=bm, bn=1024)
        p_gla = _norm_matmul(x, g_mix, w_gla, bm=bm, bn=640)
        p_rw = _norm_matmul(x, g_mix, w_rw, bm=bm, bn=1664)
        p_mg = _norm_matmul(x, g_mix, w_mg, bm=bm, bn=1024)

        gk_up_pad = jnp.pad(gla_gk_up[l], ((0, V7X_LANES - GLA_GATE_RANK), (0, 0))).astype(BF16)
        rw_prm = dict(mu=_rw_to_kernel(rwkv_mu[l]), w0=rwkv_w0[l], a0=rwkv_a0[l], kk_scale=rwkv_kk_scale[l],
                      ka=rwkv_ka[l], rk=rwkv_rk[l], ln_w=rwkv_ln_w[l], ln_b=rwkv_ln_b[l],
                      w_up=rwkv_w_up[l].astype(BF16), a_up=rwkv_a_up[l].astype(BF16),
                      g_up=rwkv_g_up[l].astype(BF16))

        yr_p, sr_p = _ret_branch(p_ret, None, n_seq=bp, t_seq=tp, chunk=min(PROMPT_CHUNK_RET, tp),
                                 n_valid=min(PROMPT_CHUNK_RET, tp), pos0=0)
        yg_p, sg_p = _gla_branch(p_gla, None, gk_up_pad, gla_gk_bias[l], gla_norm_w[l], n_seq=bp, t_seq=tp,
                                 chunk=PROMPT_CHUNK, n_valid=PROMPT_CHUNK)
        yw_p, sw_p, sh_p = _rwkv_branch(p_rw, None, None, rw_prm, n_seq=bp, t_seq=tp,
                                        chunk=PROMPT_CHUNK, n_valid=PROMPT_CHUNK)

        cs = SAMPLE_CHUNK
        pad = lambda t: _pad_time(t[n_p:], bs, ts, cs)
        unpad = lambda t: _unpad_time(t, bs, ts, cs)
        yr_s, sr_s = _ret_branch(pad(p_ret), state_ret[l], n_seq=bs, t_seq=cs, chunk=cs, n_valid=ts, pos0=past_len)
        yg_s, sg_s = _gla_branch(pad(p_gla), state_gla[l], gk_up_pad, gla_gk_bias[l], gla_norm_w[l],
                                 n_seq=bs, t_seq=cs, chunk=cs, n_valid=ts)
        yw_s, sw_s, sh_s = _rwkv_branch(pad(p_rw), state_rwkv[l], _rw_to_kernel(state_shift[l]), rw_prm,
                                        n_seq=bs, t_seq=cs, chunk=cs, n_valid=ts)

        y_ret = jnp.concatenate([yr_p[:n_p], unpad(yr_s)], axis=0)
        y_gla = jnp.concatenate([yg_p[:n_p], unpad(yg_s)], axis=0)
        y_rw = jnp.concatenate([yw_p[:n_p], unpad(yw_s)], axis=0)
        x = _merge(x, y_ret, y_gla, y_rw, p_mg, w_branch[l].astype(BF16), w_out[l].astype(BF16), bm=bm)

        j = l // 2
        if l % 2 == 0:
            x = _ffn(x, norm_ffn[l], ffn_w1[j].astype(BF16), ffn_w3[j].astype(BF16), ffn_w2[j].astype(BF16),
                     bm=bm, bf=D_FF // 2)
        else:
            router_pad = jnp.pad(moe_router[j], ((0, 0), (0, V7X_LANES - N_EXPERTS)))
            x = _moe(x, norm_ffn[l], router_pad, moe_w1[j].astype(BF16), moe_w3[j].astype(BF16),
                     moe_w2[j].astype(BF16), bm=bm, bf=D_FF // 2)

        outs['ret_p'].append(sr_p); outs['gla_p'].append(sg_p); outs['rw_p'].append(sw_p)
        outs['sh_p'].append(_rw_from_kernel(sh_p))
        outs['ret_s'].append(sr_s); outs['gla_s'].append(sg_s); outs['rw_s'].append(sw_s)
        outs['sh_s'].append(_rw_from_kernel(sh_s))

    y = _rmsnorm(x, norm_final, bm=bm)
    st = lambda k: jnp.stack(outs[k])
    return (y[:n_p].reshape(bp, tp, d), y[n_p:].reshape(bs, ts, d),
            st('ret_p'), st('gla_p'), st('rw_p'), st('sh_p'),
            st('ret_s'), st('gla_s'), st('rw_s'), st('sh_s'))
```
